```python
import math
import jax, jax.numpy as jnp
from jax import lax
import numpy as np

D_MODEL = 1024
BATCH = 8
SEQ = 4096
DEPTH = 1

EPS = 1e-6
CHUNK = 128
GM_GROUPS = 8
GM_GROUP_DIM = 64
GM_WIDTH = GM_GROUPS * GM_GROUP_DIM
WINDOW = 128
Q_HEADS = 8
KV_HEADS = 2
Q_PER_KV = Q_HEADS // KV_HEADS
HEAD_DIM = 64
ATTN_WIDTH = Q_HEADS * HEAD_DIM
KV_WIDTH = KV_HEADS * HEAD_DIM
REL_BUCKETS = 32
REL_MAX_EXACT = REL_BUCKETS // 2
REL_MAX_DIST = 128
N_GROUPS_MOE = 4
EXPERTS_PER_GROUP = 4
N_EXPERTS = N_GROUPS_MOE * EXPERTS_PER_GROUP
TOP_K = 2
EXPERT_FF = 512
MASK_VALUE = -1e30

SPLIT_SIZES = (GM_WIDTH, GM_WIDTH, ATTN_WIDTH, KV_WIDTH, KV_WIDTH, D_MODEL, D_MODEL)
SPLIT_POINTS = tuple(int(v) for v in np.cumsum(SPLIT_SIZES)[:-1])
IN_WIDTH = int(sum(SPLIT_SIZES))

kernel_name = "hybrid_gmlp_swa_sink_hiermoe_block"


def rms_norm(x, g):
    xf = x.astype(jnp.float32)
    y = xf * lax.rsqrt(jnp.mean(xf * xf, axis=-1, keepdims=True) + EPS)
    return (y * g.astype(jnp.float32)).astype(x.dtype)


def layer_norm(x, g):
    xf = x.astype(jnp.float32)
    mu = jnp.mean(xf, axis=-1, keepdims=True)
    xc = xf - mu
    y = xc * lax.rsqrt(jnp.mean(xc * xc, axis=-1, keepdims=True) + EPS)
    return (y * g.astype(jnp.float32)).astype(x.dtype)


def t5_causal_bucket(dist):
    is_small = dist < REL_MAX_EXACT
    nf = jnp.maximum(dist, 1).astype(jnp.float32)
    large = REL_MAX_EXACT + (
        jnp.log(nf / REL_MAX_EXACT) / math.log(REL_MAX_DIST / REL_MAX_EXACT)
        * (REL_BUCKETS - REL_MAX_EXACT)).astype(jnp.int32)
    large = jnp.minimum(large, REL_BUCKETS - 1)
    return jnp.where(is_small, dist, large)


def gmlp_chunk_mixer(u_raw, v_raw, v_norm_g, w_s, b_s):
    B, S, _ = u_raw.shape
    u = jax.nn.gelu(u_raw)
    v = layer_norm(jax.nn.gelu(v_raw), v_norm_g)
    v = v.reshape(B, S // CHUNK, CHUNK, GM_GROUPS, GM_GROUP_DIM)
    causal = jnp.tril(jnp.ones((CHUNK, CHUNK), dtype=bool))
    ws = jnp.where(causal[None], w_s, jnp.zeros((), w_s.dtype)).astype(v.dtype)
    sv = jnp.einsum("gts,bcsgd->bctgd", ws, v)
    sv = sv + b_s.T.astype(v.dtype)[None, None, :, :, None]
    return u * sv.reshape(B, S, GM_WIDTH)


def swa_sink_attention(q, k, v, sinks, rel_bias):
    B, S, _ = q.shape
    nb = S // WINDOW
    q = q.reshape(B, nb, WINDOW, KV_HEADS, Q_PER_KV, HEAD_DIM)
    k = k.reshape(B, nb, WINDOW, KV_HEADS, HEAD_DIM)
    v = v.reshape(B, nb, WINDOW, KV_HEADS, HEAD_DIM)
    k_prev = jnp.concatenate([jnp.zeros_like(k[:, :1]), k[:, :-1]], axis=1)
    v_prev = jnp.concatenate([jnp.zeros_like(v[:, :1]), v[:, :-1]], axis=1)
    kb = jnp.concatenate([k_prev, k], axis=2)
    vb = jnp.concatenate([v_prev, v], axis=2)

    scores = jnp.einsum("bnqkgd,bnskd->bnkgqs", q, kb,
                        preferred_element_type=jnp.float32) * (HEAD_DIM ** -0.5)

    qi = jnp.arange(WINDOW, dtype=jnp.int32)[:, None]
    sj = jnp.arange(2 * WINDOW, dtype=jnp.int32)[None, :]
    dist = qi + WINDOW - sj
    in_window = (dist >= 0) & (dist < WINDOW)
    bucket = t5_causal_bucket(jnp.clip(dist, 0, WINDOW - 1))
    bias = rel_bias.astype(jnp.float32)[bucket]
    bias = bias.transpose(2, 0, 1).reshape(KV_HEADS, Q_PER_KV, WINDOW, 2 * WINDOW)
    blk = jnp.arange(nb, dtype=jnp.int32)[:, None, None]
    valid = in_window[None] & ((blk > 0) | (sj[None] >= WINDOW))

    scores = scores + bias[None, None]
    scores = jnp.where(valid[None, :, None, None], scores, MASK_VALUE)
    sink = sinks.astype(jnp.float32).reshape(KV_HEADS, Q_PER_KV)[None, None, :, :, None, None]
    m = jnp.maximum(jnp.max(scores, axis=-1, keepdims=True), sink)
    p = jnp.exp(scores - m)
    denom = jnp.sum(p, axis=-1, keepdims=True) + jnp.exp(sink - m)
    probs = (p / denom).astype(vb.dtype)
    out = jnp.einsum("bnkgqs,bnskd->bnqkgd", probs, vb)
    return out.reshape(B, S, ATTN_WIDTH)


def hierarchical_moe(h, rg_w, rg_b, re_w, re_b, w_gate, w_up, w_down):
    B, S, D = h.shape
    t = h.reshape(-1, D)
    g_logits = jnp.dot(t, rg_w, preferred_element_type=jnp.float32) + rg_b.astype(jnp.float32)
    g_probs = jax.nn.softmax(g_logits, axis=-1)
    g_top_p, g_top_i = lax.top_k(g_probs, 1)
    e_all = jnp.einsum("td,gde->tge", t, re_w,
                       preferred_element_type=jnp.float32) + re_b.astype(jnp.float32)
    e_logits = jnp.take_along_axis(e_all, g_top_i[:, :, None], axis=1)[:, 0]
    e_top_v, e_top_i = lax.top_k(e_logits, TOP_K)
    w_top = jax.nn.softmax(e_top_v, axis=-1) * g_top_p
    expert_id = g_top_i * EXPERTS_PER_GROUP + e_top_i
    combine = jnp.einsum("tk,tke->te", w_top,
                         jax.nn.one_hot(expert_id, N_EXPERTS, dtype=jnp.float32))
    combine = combine.astype(t.dtype)
    out = jnp.zeros_like(t)
    for e in range(N_EXPERTS):
        hid = jax.nn.silu(t @ w_gate[e]) * (t @ w_up[e])
        out = out + combine[:, e:e + 1] * (hid @ w_down[e])
    return out.reshape(B, S, D)


def setup_inputs(seed: int = 0) -> dict:
    key = jax.random.key(seed)
    ks = jax.random.split(key, 20)
    f32 = jnp.float32
    nrm = lambda k, shape, scale: jax.random.normal(k, shape, f32) * scale
    L = DEPTH
    return {
        "x": jax.random.normal(ks[0], (BATCH, SEQ, D_MODEL), f32),
        "attn_norm_g": 1.0 + nrm(ks[1], (L, D_MODEL), 0.02),
        "w_in": nrm(ks[2], (L, D_MODEL, IN_WIDTH), D_MODEL ** -0.5),
        "gm_v_norm_g": 1.0 + nrm(ks[3], (L, GM_WIDTH), 0.02),
        "gm_w_spatial": nrm(ks[4], (L, GM_GROUPS, CHUNK, CHUNK), CHUNK ** -0.5),
        "gm_b_spatial": 1.0 + nrm(ks[5], (L, GM_GROUPS, CHUNK), 0.02),
        "attn_sinks": nrm(ks[6], (L, Q_HEADS), 0.5),
        "rel_bias": nrm(ks[7], (REL_BUCKETS, Q_HEADS), 0.5),
        "w_proj_a": nrm(ks[8], (L, GM_WIDTH, D_MODEL), GM_WIDTH ** -0.5),
        "w_proj_b": nrm(ks[9], (L, ATTN_WIDTH, D_MODEL), ATTN_WIDTH ** -0.5),
        "w_out": nrm(ks[10], (L, D_MODEL, D_MODEL), D_MODEL ** -0.5),
        "ffn_norm_g": 1.0 + nrm(ks[11], (L, D_MODEL), 0.02),
        "router_group_w": nrm(ks[12], (L, D_MODEL, N_GROUPS_MOE), D_MODEL ** -0.5),
        "router_group_b": nrm(ks[13], (L, N_GROUPS_MOE), 0.01),
        "router_expert_w": nrm(ks[14], (L, N_GROUPS_MOE, D_MODEL, EXPERTS_PER_GROUP), D_MODEL ** -0.5),
        "router_expert_b": nrm(ks[15], (L, N_GROUPS_MOE, EXPERTS_PER_GROUP), 0.01),
        "expert_w_gate": nrm(ks[16], (L, N_EXPERTS, D_MODEL, EXPERT_FF), D_MODEL ** -0.5),
        "expert_w_up": nrm(ks[17], (L, N_EXPERTS, D_MODEL, EXPERT_FF), D_MODEL ** -0.5),
        "expert_w_down": nrm(ks[18], (L, N_EXPERTS, EXPERT_FF, D_MODEL), EXPERT_FF ** -0.5),
        "final_norm_g": 1.0 + nrm(ks[19], (D_MODEL,), 0.02),
    }


def reference(x, attn_norm_g, w_in, gm_v_norm_g, gm_w_spatial, gm_b_spatial, attn_sinks,
              rel_bias, w_proj_a, w_proj_b, w_out, ffn_norm_g, router_group_w, router_group_b,
              router_expert_w, router_expert_b, expert_w_gate, expert_w_up, expert_w_down,
              final_norm_g):
    h = x
    for l in range(DEPTH):
        hn = rms_norm(h, attn_norm_g[l])
        proj = hn @ w_in[l]
        u_raw, v_raw, q, k, v, gate_a, gate_b = jnp.split(proj, SPLIT_POINTS, axis=-1)
        a = gmlp_chunk_mixer(u_raw, v_raw, gm_v_norm_g[l], gm_w_spatial[l], gm_b_spatial[l])
        b = swa_sink_attention(q, k, v, attn_sinks[l], rel_bias)
        merged = (jax.nn.sigmoid(gate_a) * (a @ w_proj_a[l])
                  + jax.nn.sigmoid(gate_b) * (b @ w_proj_b[l]))
        h = h + merged @ w_out[l]
        hn = rms_norm(h, ffn_norm_g[l])
        h = h + hierarchical_moe(hn, router_group_w[l], router_group_b[l], router_expert_w[l],
                                 router_expert_b[l], expert_w_gate[l], expert_w_up[l],
                                 expert_w_down[l])
    return rms_norm(h, final_norm_g)
```

```python
import functools

import jax
import jax.numpy as jnp
import numpy as np
from jax import lax
from jax.experimental import pallas as pl
from jax.experimental.pallas import tpu as pltpu

EPS = 1e-6
CHUNK = 128
GM_GROUPS = 8
GM_GROUP_DIM = 64
GM_WIDTH = GM_GROUPS * GM_GROUP_DIM
Q_HEADS = 8
KV_HEADS = 2
Q_PER_KV = Q_HEADS // KV_HEADS
HEAD_DIM = 64
ATTN_WIDTH = Q_HEADS * HEAD_DIM
KV_WIDTH = KV_HEADS * HEAD_DIM
REL_BUCKETS = 32
REL_MAX_EXACT = REL_BUCKETS // 2
REL_MAX_DIST = 128
N_GROUPS_MOE = 4
EXPERTS_PER_GROUP = 4
N_EXPERTS = N_GROUPS_MOE * EXPERTS_PER_GROUP
PAIRS_PER_GROUP = 6
N_CLASSES = N_GROUPS_MOE * PAIRS_PER_GROUP
MASK_VALUE = -1e30

LANES = 128
ROUTE_LANES = LANES
TOKEN_TILE = 512
MOE_ROW_TILE = 256
MIXER_VMEM_BYTES = 56 * 1024 * 1024
MOE_VMEM_BYTES = 40 * 1024 * 1024

_F32 = jnp.float32
_BF16 = jnp.bfloat16


def _dot(a, b):
    return jnp.dot(a, b, preferred_element_type=_F32)


def _dot_nt(a, b):
    return lax.dot_general(a, b, (((1,), (1,)), ((), ())), preferred_element_type=_F32)


def _rms(x, g):
    return x * lax.rsqrt(jnp.mean(x * x, axis=-1, keepdims=True) + EPS) * g


def _mixer_kernel(sinks_ref, relb_ref,
                  x_ref, ang_ref, win_ref, gvn_ref, ws_ref, bsp_ref, bucket_ref,
                  wpa_ref, wpb_ref, wout_ref, fng_ref, wr_ref, br_ref,
                  hext_ref, cls_ref,
                  wcat_scr, bias_scr, kv_scr, *, tiles_per_seq):
    i = pl.program_id(0)
    ts, d_model = x_ref.shape
    n_blk = ts // CHUNK

    @pl.when(i == 0)
    def _init_tables():
        row = lax.broadcasted_iota(jnp.int32, (CHUNK, CHUNK), 0)
        col = lax.broadcasted_iota(jnp.int32, (CHUNK, CHUNK), 1)
        causal = col <= row
        for g in range(GM_GROUPS):
            w = jnp.where(causal, ws_ref[g], 0.0).astype(_BF16)
            wcat_scr[g // 2, :, (g % 2) * CHUNK:(g % 2 + 1) * CHUNK] = w
        qi = lax.broadcasted_iota(jnp.int32, (CHUNK, 2 * CHUNK), 0)
        sj = lax.broadcasted_iota(jnp.int32, (CHUNK, 2 * CHUNK), 1)
        dist = qi + CHUNK - sj
        in_window = (dist >= 0) & (dist < CHUNK)
        own_block = in_window & (sj >= CHUNK)
        bucket = bucket_ref[...]
        hit = [bucket == b for b in range(REL_BUCKETS)]
        for h in range(Q_HEADS):
            acc = jnp.zeros((CHUNK, 2 * CHUNK), _F32)
            for b in range(REL_BUCKETS):
                acc = jnp.where(hit[b], relb_ref[b * Q_HEADS + h], acc)
            n, g = divmod(h, Q_PER_KV)
            bias_scr[0, n, g * CHUNK:(g + 1) * CHUNK, :] = jnp.where(in_window, acc, MASK_VALUE)
            bias_scr[1, n, g * CHUNK:(g + 1) * CHUNK, :] = jnp.where(own_block, acc, MASK_VALUE)

    seq_start = (i % tiles_per_seq) == 0

    @pl.when(seq_start)
    def _reset_carry():
        kv_scr[...] = jnp.zeros(kv_scr.shape, kv_scr.dtype)

    x = x_ref[...]
    hn = _rms(x, ang_ref[...]).astype(_BF16)

    uv = _dot(hn, win_ref[:, 0:2 * GM_WIDTH])
    u = jax.nn.gelu(uv[:, :GM_WIDTH])
    vg = jax.nn.gelu(uv[:, GM_WIDTH:])
    mu = jnp.mean(vg, axis=-1, keepdims=True)
    vc = vg - mu
    v = vc * lax.rsqrt(jnp.mean(vc * vc, axis=-1, keepdims=True) + EPS) * gvn_ref[...]
    lane = lax.broadcasted_iota(jnp.int32, (CHUNK, LANES), 1)
    low_half = lane < GM_GROUP_DIM
    bsp = bsp_ref[...]
    a_rows = []
    for c in range(n_blk):
        sv_pairs = []
        for p in range(GM_GROUPS // 2):
            vp = v[c * CHUNK:(c + 1) * CHUNK, p * LANES:(p + 1) * LANES]
            rhs = jnp.concatenate([jnp.where(low_half, vp, 0.0), jnp.where(low_half, 0.0, vp)],
                                  axis=0).astype(_BF16)
            sv_pairs.append(_dot(wcat_scr[p], rhs))
        sv = jnp.concatenate(sv_pairs, axis=1) + bsp
        a_rows.append(u[c * CHUNK:(c + 1) * CHUNK, :] * sv)
    a = jnp.concatenate(a_rows, axis=0).astype(_BF16)
    a_out = _dot(a, wpa_ref[...])

    q_off = 2 * GM_WIDTH
    k_off = q_off + ATTN_WIDTH
    qkv = _dot(hn, win_ref[:, q_off:k_off + 2 * KV_WIDTH])
    q = qkv[:, :ATTN_WIDTH] * (HEAD_DIM ** -0.5)
    k_all = qkv[:, ATTN_WIDTH:ATTN_WIDTH + KV_WIDTH].astype(_BF16)
    v_all = qkv[:, ATTN_WIDTH + KV_WIDTH:].astype(_BF16)
    kv_prev = kv_scr[...]
    k_prev, v_prev = kv_prev[:, :KV_WIDTH], kv_prev[:, KV_WIDTH:]
    lane2 = lax.broadcasted_iota(jnp.int32, (2 * CHUNK, LANES), 1)
    low2 = lane2 < HEAD_DIM
    b_rows = []
    for blk in range(n_blk):
        rows = slice(blk * CHUNK, (blk + 1) * CHUNK)
        k_blk, v_blk = k_all[rows], v_all[rows]
        kb = jnp.concatenate([k_prev, k_blk], axis=0)
        vb = jnp.concatenate([v_prev, v_blk], axis=0)
        kb_sw = pltpu.roll(kb, HEAD_DIM, 1)
        vb_sw = pltpu.roll(vb, HEAD_DIM, 1)
        out_pairs = [None] * (Q_HEADS // 2)
        for n in range(KV_HEADS):
            k_dup = jnp.where(low2, kb, kb_sw) if n == 0 else jnp.where(low2, kb_sw, kb)
            v_dup = jnp.where(low2, vb, vb_sw) if n == 0 else jnp.where(low2, vb_sw, vb)
            q_stack = []
            for g in range(Q_PER_KV):
                h = n * Q_PER_KV + g
                qp = q[rows, (h // 2) * LANES:(h // 2 + 1) * LANES]
                keep = low_half if h % 2 == 0 else jnp.logical_not(low_half)
                q_stack.append(jnp.where(keep, qp, 0.0))
            qs = jnp.concatenate(q_stack, axis=0).astype(_BF16)
            s = _dot_nt(qs, k_dup)
            if blk == 0:
                s = s + bias_scr[jnp.where(seq_start, 1, 0), n]
            else:
                s = s + bias_scr[0, n]
            outs = []
            for g in range(Q_PER_KV):
                h = n * Q_PER_KV + g
                sg = s[g * CHUNK:(g + 1) * CHUNK]
                sink = sinks_ref[h]
                m = jnp.maximum(jnp.max(sg, axis=-1, keepdims=True), sink)
                pr = jnp.exp(sg - m)
                denom = jnp.sum(pr, axis=-1, keepdims=True) + jnp.exp(sink - m)
                og = _dot(pr.astype(_BF16), v_dup)
                outs.append(og / denom)
            for g in range(0, Q_PER_KV, 2):
                h = n * Q_PER_KV + g
                out_pairs[h // 2] = jnp.where(low_half, outs[g], outs[g + 1])
        b_rows.append(jnp.concatenate(out_pairs, axis=1))
        k_prev, v_prev = k_blk, v_blk
    kv_scr[:, :KV_WIDTH] = k_prev
    kv_scr[:, KV_WIDTH:] = v_prev
    b = jnp.concatenate(b_rows, axis=0).astype(_BF16)
    b_out = _dot(b, wpb_ref[...])

    g_off = k_off + 2 * KV_WIDTH
    gates = _dot(hn, win_ref[:, g_off:g_off + 2 * d_model])
    merged = (jax.nn.sigmoid(gates[:, :d_model]) * a_out
              + jax.nn.sigmoid(gates[:, d_model:]) * b_out).astype(_BF16)
    h = x + _dot(merged, wout_ref[...])
    hext_ref[:, :d_model] = h

    hn2 = _rms(h, fng_ref[...])
    hi = hn2.astype(_BF16)
    lo = (hn2 - hi.astype(_F32)).astype(_BF16)
    l2 = _dot(jnp.concatenate([hi, lo], axis=1), wr_ref[...])
    logits = l2[:, :LANES] + l2[:, LANES:] + br_ref[...]
    lt = logits.T
    r = [lt[j:j + 1, :] for j in range(N_GROUPS_MOE + N_EXPERTS)]
    gl = r[:N_GROUPS_MOE]
    gmax = functools.reduce(jnp.maximum, gl)
    ge = [jnp.exp(t - gmax) for t in gl]
    gz = functools.reduce(lambda p_, q_: p_ + q_, ge)
    gp = [t / gz for t in ge]
    g_idx = jnp.zeros(gmax.shape, jnp.int32)
    g_top = gp[0]
    for j in range(1, N_GROUPS_MOE):
        better = gp[j] > g_top
        g_idx = jnp.where(better, j, g_idx)
        g_top = jnp.where(better, gp[j], g_top)
    el = []
    for j in range(EXPERTS_PER_GROUP):
        t = r[N_GROUPS_MOE + j]
        for g in range(1, N_GROUPS_MOE):
            t = jnp.where(g_idx == g, r[N_GROUPS_MOE + g * EXPERTS_PER_GROUP + j], t)
        el.append(t)
    i1 = jnp.zeros(gmax.shape, jnp.int32)
    v1 = el[0]
    for j in range(1, EXPERTS_PER_GROUP):
        better = el[j] > v1
        i1 = jnp.where(better, j, i1)
        v1 = jnp.where(better, el[j], v1)
    i2 = jnp.where(i1 == 0, 1, 0)
    v2 = jnp.where(i1 == 0, el[1], el[0])
    for j in range(1, EXPERTS_PER_GROUP):
        better = (el[j] > v2) & (i1 != j) & (i2 != j)
        i2 = jnp.where(better, j, i2)
        v2 = jnp.where(better, el[j], v2)
    e2 = jnp.exp(v2 - v1)
    z2 = 1.0 + e2
    w_first = (1.0 / z2) * g_top
    w_second = (e2 / z2) * g_top
    first_is_low = i1 < i2
    e_lo = jnp.minimum(i1, i2)
    e_hi = jnp.maximum(i1, i2)
    w_lo = jnp.where(first_is_low, w_first, w_second)
    w_hi = jnp.where(first_is_low, w_second, w_first)
    pair = jnp.where(e_lo == 0, 0, jnp.where(e_lo == 1, 3, 5)) + e_hi - e_lo - 1
    cls = g_idx * PAIRS_PER_GROUP + pair
    cls_ref[0] = cls
    rid = lax.broadcasted_iota(jnp.int32, (LANES, ts), 0)
    route_t = jnp.where(rid == 0, w_lo, jnp.where(rid == 1, w_hi, 0.0))
    hext_ref[:, d_model:] = route_t.T


def _const_spec(shape):
    nd = len(shape)
    return pl.BlockSpec(shape, lambda *_: (0,) * nd, pipeline_mode=pl.Buffered(1))


def _mixer_call(x2d, tiles_per_seq, ts, sinks, relb, ang, win, gvn, ws, bsp, bucket, wpa, wpb, wout,
                fng, wr, br):
    t_tokens, d_model = x2d.shape
    n_tiles = t_tokens // ts
    smem = pl.BlockSpec(memory_space=pltpu.SMEM)
    in_specs = [
        smem, smem,
        pl.BlockSpec((ts, d_model), lambda i: (i, 0)),
        _const_spec(ang.shape), _const_spec(win.shape), _const_spec(gvn.shape), _const_spec(ws.shape),
        _const_spec(bsp.shape), _const_spec(bucket.shape), _const_spec(wpa.shape), _const_spec(wpb.shape),
        _const_spec(wout.shape), _const_spec(fng.shape), _const_spec(wr.shape), _const_spec(br.shape),
    ]
    out_shape = (jax.ShapeDtypeStruct((t_tokens, d_model + ROUTE_LANES), _F32),
                 jax.ShapeDtypeStruct((n_tiles, 1, ts), jnp.int32))
    out_specs = (pl.BlockSpec((ts, d_model + ROUTE_LANES), lambda i: (i, 0)),
                 pl.BlockSpec((1, 1, ts), lambda i: (i, 0, 0)))
    scratch = [pltpu.VMEM((GM_GROUPS // 2, CHUNK, 2 * CHUNK), _BF16),
               pltpu.VMEM((2, KV_HEADS, Q_PER_KV * CHUNK, 2 * CHUNK), _F32),
               pltpu.VMEM((CHUNK, 2 * KV_WIDTH), _BF16)]
    return pl.pallas_call(
        functools.partial(_mixer_kernel, tiles_per_seq=tiles_per_seq),
        grid=(n_tiles,), in_specs=in_specs, out_specs=out_specs, out_shape=out_shape,
        scratch_shapes=scratch, name="token_mixer",
        compiler_params=pltpu.CompilerParams(dimension_semantics=("arbitrary",),
                                             vmem_limit_bytes=MIXER_VMEM_BYTES),
    )(sinks, relb, x2d, ang, win, gvn, ws, bsp, bucket, wpa, wpb, wout, fng, wr, br)


def _plan_dest_kernel(cls_ref, dest_ref, cnt_ref, *, row_tile):
    n_rows, n_cols = cls_ref.shape
    cls = cls_ref[...]
    kk = lax.broadcasted_iota(jnp.int32, (n_cols, n_cols), 0)
    ll = lax.broadcasted_iota(jnp.int32, (n_cols, n_cols), 1)
    before_col = jnp.where(kk < ll, 1.0, 0.0).astype(_BF16)
    rr = lax.broadcasted_iota(jnp.int32, (n_rows, n_rows), 0)
    cc = lax.broadcasted_iota(jnp.int32, (n_rows, n_rows), 1)
    before_row = jnp.where(cc < rr, 1.0, 0.0).astype(_BF16)
    lane = lax.broadcasted_iota(jnp.int32, (1, LANES), 1)
    shift = int(np.log2(row_tile))
    dest = jnp.zeros((n_rows, n_cols), jnp.int32)
    cnt_row = jnp.zeros((1, LANES), jnp.int32)
    base = jnp.zeros((1, 1), jnp.int32)
    for c in range(N_CLASSES):
        member = cls == c
        m = jnp.where(member, 1.0, 0.0)
        within = _dot(m.astype(_BF16), before_col)
        tot = jnp.sum(m, axis=1, keepdims=True)
        tot_hi = jnp.floor(tot * (1.0 / 16.0))
        tot_lo = tot - 16.0 * tot_hi
        th = jnp.broadcast_to(tot_hi, (n_rows, LANES)).astype(_BF16)
        tl = jnp.broadcast_to(tot_lo, (n_rows, LANES)).astype(_BF16)
        row_off = 16.0 * _dot(before_row, th) + _dot(before_row, tl)
        cnt = jnp.sum(tot, axis=0, keepdims=True).astype(jnp.int32)
        rank = (within + row_off[:, 0:1]).astype(jnp.int32)
        dest = jnp.where(member, base + rank, dest)
        cnt_row = jnp.where(lane == c, cnt, cnt_row)
        padded = ((cnt + (row_tile - 1)) >> shift) << shift
        base = base + padded
    dest_ref[...] = dest
    cnt_ref[...] = cnt_row


def _plan_dest_call(cls2d, row_tile):
    n_rows, n_cols = cls2d.shape
    return pl.pallas_call(
        functools.partial(_plan_dest_kernel, row_tile=row_tile),
        out_shape=(jax.ShapeDtypeStruct((n_rows, n_cols), jnp.int32),
                   jax.ShapeDtypeStruct((1, LANES), jnp.int32)),
        name="route_plan_dest",
    )(cls2d)


_PAIR_LO = (0, 0, 0, 1, 1, 2)
_PAIR_HI = (1, 2, 3, 2, 3, 3)


def _plan_src_kernel(dest_ref, cnt_ref, src_ref, e_lo_ref, e_hi_ref, nvalid_ref, ntiles_ref, *, row_tile):
    n_rows, n_cols = dest_ref.shape
    shift = int(np.log2(row_tile))
    max_tiles = e_lo_ref.shape[0]

    base = jnp.int32(0)
    last_lo = jnp.int32(0)
    last_hi = jnp.int32(1)
    for c in range(N_CLASSES):
        g, p = divmod(c, PAIRS_PER_GROUP)
        lo_id = g * EXPERTS_PER_GROUP + _PAIR_LO[p]
        hi_id = g * EXPERTS_PER_GROUP + _PAIR_HI[p]
        cnt = cnt_ref[0, c]
        n_t = (cnt + (row_tile - 1)) >> shift
        first_tile = base >> shift

        def _tile(k, carry, lo_id=lo_id, hi_id=hi_id, first_tile=first_tile, cnt=cnt):
            e_lo_ref[first_tile + k] = lo_id
            e_hi_ref[first_tile + k] = hi_id
            nvalid_ref[first_tile + k] = jnp.minimum(row_tile, cnt - k * row_tile)
            return carry
        lax.fori_loop(0, n_t, _tile, 0)

        def _pad(s, carry):
            src_ref[s >> 7, s & (LANES - 1)] = 0
            return carry
        lax.fori_loop(base + cnt, base + (n_t << shift), _pad, 0)
        used = cnt > 0
        last_lo = jnp.where(used, lo_id, last_lo)
        last_hi = jnp.where(used, hi_id, last_hi)
        base = base + (n_t << shift)

    n_tiles = base >> shift
    ntiles_ref[0] = n_tiles

    def _tail(t, carry):
        e_lo_ref[t] = last_lo
        e_hi_ref[t] = last_hi
        nvalid_ref[t] = 0
        return carry
    lax.fori_loop(n_tiles, max_tiles, _tail, 0)

    def _tail_rows(rw, carry):
        for cl in range(LANES):
            src_ref[rw, cl] = 0
        return carry
    lax.fori_loop(base >> 7, src_ref.shape[0], _tail_rows, 0)

    def _row(rw, carry):
        def _col(cl, carry2):
            d = dest_ref[rw, cl]
            src_ref[d >> 7, d & (LANES - 1)] = rw * n_cols + cl
            return carry2
        return lax.fori_loop(0, n_cols, _col, carry, unroll=8)
    lax.fori_loop(0, n_rows, _row, 0)


def _plan_src_call(dest2d, cnt_row, row_tile):
    n_rows, n_cols = dest2d.shape
    t_tokens = n_rows * n_cols
    max_tiles = t_tokens // row_tile + N_CLASSES
    n_slots = max_tiles * row_tile
    smem = pl.BlockSpec(memory_space=pltpu.SMEM)
    return pl.pallas_call(
        functools.partial(_plan_src_kernel, row_tile=row_tile),
        in_specs=[smem, smem],
        out_specs=(smem, smem, smem, smem, smem),
        out_shape=(jax.ShapeDtypeStruct((n_slots // LANES, LANES), jnp.int32),
                   jax.ShapeDtypeStruct((max_tiles,), jnp.int32),
                   jax.ShapeDtypeStruct((max_tiles,), jnp.int32),
                   jax.ShapeDtypeStruct((max_tiles,), jnp.int32),
                   jax.ShapeDtypeStruct((1,), jnp.int32)),
        name="route_plan_src",
    )(dest2d, cnt_row)


def _moe_kernel(src_ref, e_lo_ref, e_hi_ref, nvalid_ref, ntiles_ref,
                hext_hbm, fng_ref, ong_ref,
                wg_lo_ref, wu_lo_ref, wd_lo_ref, wg_hi_ref, wu_hi_ref, wd_hi_ref,
                out_hbm,
                xbuf, ybuf, gsem, ssem, *, row_tile):
    i = pl.program_id(0)
    n_tiles = ntiles_ref[0]
    d_model = out_hbm.shape[1]
    rows_per_srow = LANES
    srows = row_tile // rows_per_srow
    slot = i % 2

    def token_of(tile, r):
        return src_ref[tile * srows + r // rows_per_srow, r % rows_per_srow]

    def gather_copy(tok, r, buf_slot):
        return pltpu.make_async_copy(hext_hbm.at[pl.ds(tok, 1), :], xbuf.at[buf_slot, pl.ds(r, 1), :],
                                     gsem.at[buf_slot])

    def scatter_copy(tok, r, buf_slot):
        return pltpu.make_async_copy(ybuf.at[buf_slot, pl.ds(r, 1), :], out_hbm.at[pl.ds(tok, 1), :],
                                     ssem.at[buf_slot])

    def start_gather(tile, buf_slot):
        def body(r, carry):
            gather_copy(token_of(tile, r), r, buf_slot).start()
            return carry
        lax.fori_loop(0, row_tile, body, 0, unroll=8)

    def wait_gather(buf_slot):
        def body(r, carry):
            gather_copy(0, r, buf_slot).wait()
            return carry
        lax.fori_loop(0, row_tile, body, 0, unroll=8)

    def start_scatter(tile, buf_slot, n_rows):
        def body(r, carry):
            scatter_copy(token_of(tile, r), r, buf_slot).start()
            return carry
        lax.fori_loop(0, n_rows, body, 0)

    def wait_scatter(buf_slot, n_rows):
        def body(r, carry):
            scatter_copy(0, r, buf_slot).wait()
            return carry
        lax.fori_loop(0, n_rows, body, 0)

    @pl.when(i < n_tiles)
    def _tile():
        @pl.when(i == 0)
        def _prime():
            start_gather(0, 0)

        wait_gather(slot)

        @pl.when(i + 1 < n_tiles)
        def _prefetch():
            start_gather(i + 1, 1 - slot)

        xe = xbuf[slot]
        h = xe[:, :d_model]
        w_lo = xe[:, d_model:d_model + 1]
        w_hi = xe[:, d_model + 1:d_model + 2]
        hn = _rms(h, fng_ref[...]).astype(_BF16)

        def expert(wg_ref, wu_ref, wd_ref):
            gate = _dot(hn, wg_ref[0])
            up = _dot(hn, wu_ref[0])
            hid = (jax.nn.silu(gate) * up).astype(_BF16)
            return _dot(hid, wd_ref[0])

        y = h + w_lo * expert(wg_lo_ref, wu_lo_ref, wd_lo_ref)
        y = y + w_hi * expert(wg_hi_ref, wu_hi_ref, wd_hi_ref)
        ybuf[slot] = _rms(y, ong_ref[...])

        start_scatter(i, slot, nvalid_ref[i])

        @pl.when(i > 0)
        def _drain_prev():
            wait_scatter(1 - slot, nvalid_ref[jnp.maximum(i - 1, 0)])

        @pl.when(i == n_tiles - 1)
        def _drain_last():
            wait_scatter(slot, nvalid_ref[i])


def _moe_call(src2d, e_lo, e_hi, nvalid, ntiles, hext, fng, ong, wg, wu, wd, row_tile):
    t_tokens = hext.shape[0]
    d_model = hext.shape[1] - ROUTE_LANES
    max_tiles = e_lo.shape[0]
    ff = wg.shape[2]

    def lo_map(i, src, elo, ehi, nv, nt):
        return (elo[i], 0, 0)

    def hi_map(i, src, elo, ehi, nv, nt):
        return (ehi[i], 0, 0)

    vec = pl.BlockSpec((1, d_model), lambda i, *_: (0, 0))
    grid_spec = pltpu.PrefetchScalarGridSpec(
        num_scalar_prefetch=5,
        grid=(max_tiles,),
        in_specs=[pl.BlockSpec(memory_space=pl.ANY), vec, vec,
                  pl.BlockSpec((1, d_model, ff), lo_map), pl.BlockSpec((1, d_model, ff), lo_map),
                  pl.BlockSpec((1, ff, d_model), lo_map),
                  pl.BlockSpec((1, d_model, ff), hi_map), pl.BlockSpec((1, d_model, ff), hi_map),
                  pl.BlockSpec((1, ff, d_model), hi_map)],
        out_specs=pl.BlockSpec(memory_space=pl.ANY),
        scratch_shapes=[pltpu.VMEM((2, row_tile, d_model + ROUTE_LANES), _F32),
                        pltpu.VMEM((2, row_tile, d_model), _F32),
                        pltpu.SemaphoreType.DMA((2,)), pltpu.SemaphoreType.DMA((2,))],
    )
    return pl.pallas_call(
        functools.partial(_moe_kernel, row_tile=row_tile),
        grid_spec=grid_spec,
        out_shape=jax.ShapeDtypeStruct((t_tokens, d_model), _F32),
        name="moe_experts",
        compiler_params=pltpu.CompilerParams(dimension_semantics=("arbitrary",),
                                             vmem_limit_bytes=MOE_VMEM_BYTES),
    )(src2d, e_lo, e_hi, nvalid, ntiles, hext, fng, ong, wg, wu, wd, wg, wu, wd)


def _t5_bucket_table():
    qi = jnp.arange(CHUNK, dtype=jnp.int32)[:, None]
    sj = jnp.arange(2 * CHUNK, dtype=jnp.int32)[None, :]
    dist = jnp.clip(qi + CHUNK - sj, 0, CHUNK - 1)
    nf = jnp.maximum(dist, 1).astype(_F32)
    large = REL_MAX_EXACT + (jnp.log(nf / REL_MAX_EXACT) / np.log(REL_MAX_DIST / REL_MAX_EXACT)
                             * (REL_BUCKETS - REL_MAX_EXACT)).astype(jnp.int32)
    large = jnp.minimum(large, REL_BUCKETS - 1)
    return jnp.where(dist < REL_MAX_EXACT, dist, large)


def _forward(x, attn_norm_g, w_in, gm_v_norm_g, gm_w_spatial, gm_b_spatial, attn_sinks, rel_bias,
             w_proj_a, w_proj_b, w_out, ffn_norm_g, router_group_w, router_group_b, router_expert_w,
             router_expert_b, expert_w_gate, expert_w_up, expert_w_down, final_norm_g,
             token_tile=TOKEN_TILE, row_tile=MOE_ROW_TILE):
    batch, seq, d_model = x.shape
    depth = attn_norm_g.shape[0]
    assert seq % token_tile == 0 and token_tile % CHUNK == 0
    h = x
    for l in range(depth):
        last = l == depth - 1
        wr = jnp.concatenate(
            [router_group_w[l], router_expert_w[l].transpose(1, 0, 2).reshape(d_model, N_EXPERTS)], axis=1)
        wr = jnp.pad(wr, ((0, 0), (0, LANES - wr.shape[1])))
        wr_hi = wr.astype(_BF16)
        wr_lo = (wr - wr_hi.astype(_F32)).astype(_BF16)
        wr_stack = jnp.concatenate(
            [jnp.concatenate([wr_hi, wr_lo], axis=1),
             jnp.concatenate([wr_hi, jnp.zeros_like(wr_hi)], axis=1)], axis=0)
        br = jnp.pad(jnp.concatenate([router_group_b[l], router_expert_b[l].reshape(-1)]),
                     (0, LANES - N_GROUPS_MOE - N_EXPERTS))[None, :]
        bsp = jnp.repeat(gm_b_spatial[l].T, GM_GROUP_DIM, axis=1)
        hext, cls = _mixer_call(
            h.reshape(batch * seq, d_model), seq // token_tile, token_tile,
            attn_sinks[l], rel_bias.reshape(-1), attn_norm_g[l][None, :], w_in[l].astype(_BF16),
            gm_v_norm_g[l][None, :], gm_w_spatial[l], bsp, _t5_bucket_table(),
            w_proj_a[l].astype(_BF16), w_proj_b[l].astype(_BF16), w_out[l].astype(_BF16),
            ffn_norm_g[l][None, :], wr_stack, br)
        dest, cnt_row = _plan_dest_call(cls.reshape(cls.shape[0], cls.shape[2]), row_tile)
        src2d, e_lo, e_hi, nvalid, ntiles = _plan_src_call(dest, cnt_row, row_tile)
        out_gain = final_norm_g if last else jnp.ones_like(final_norm_g)
        out = _moe_call(src2d, e_lo, e_hi, nvalid, ntiles, hext, ffn_norm_g[l][None, :],
                        out_gain[None, :], expert_w_gate[l].astype(_BF16), expert_w_up[l].astype(_BF16),
                        expert_w_down[l].astype(_BF16), row_tile)
        assert last, "multi-layer stacks need the un-normalised residual between layers"
        h = out.reshape(batch, seq, d_model)
    return h


def kernel(x, attn_norm_g, w_in, gm_v_norm_g, gm_w_spatial, gm_b_spatial, attn_sinks, rel_bias, w_proj_a,
           w_proj_b, w_out, ffn_norm_g, router_group_w, router_group_b, router_expert_w, router_expert_b,
           expert_w_gate, expert_w_up, expert_w_down, final_norm_g):
    return _forward(x, attn_norm_g, w_in, gm_v_norm_g, gm_w_spatial, gm_b_spatial, attn_sinks, rel_bias,
                    w_proj_a, w_proj_b, w_out, ffn_norm_g, router_group_w, router_group_b,
                    router_expert_w, router_expert_b, expert_w_gate, expert_w_up, expert_w_down,
                    final_norm_g)
```
